```python
import jax, jax.numpy as jnp
from jax import lax
import numpy as np

D_MODEL = 2048
BATCH = 16
SEQ = 2048
DEPTH = 2

N_MIXERS = 2
N_EVEN = (DEPTH + 1) // 2
N_ODD = DEPTH // 2
CONV_WIDTH = 3
HEAD_DIM = 128
N_HEADS = D_MODEL // HEAD_DIM
Q_BLOCK = 128
D_FF = 5632
N_EXPERTS = 8
TOP_K = 2
D_FF_EXPERT = (7 * D_MODEL) // 2
N_MOD = 6
EPS = 1e-6

kernel_name = "hybrid_shortconv_fox_moe_adaln"


def rms_norm(x, g):
    xf = x.astype(jnp.float32)
    n = xf * lax.rsqrt(jnp.mean(xf * xf, axis=-1, keepdims=True) + EPS)
    return (n * g.astype(jnp.float32)).astype(x.dtype)


def modulate(n, shift, scale):
    return n * (1 + scale[:, None, :]) + shift[:, None, :]


def short_conv_mixer(h, w_in, conv_w, w_out):
    bcx = h @ w_in
    b_gate, c_gate, xv = jnp.split(bcx, 3, axis=-1)
    u = c_gate * xv
    conv = lax.conv_general_dilated(
        u, conv_w[:, None, :].astype(u.dtype),
        window_strides=(1,),
        padding=((CONV_WIDTH - 1, 0),),
        dimension_numbers=("NWC", "WIO", "NWC"),
        feature_group_count=D_MODEL)
    return (b_gate * conv) @ w_out


def forgetting_attention(h, w_in, b_f, w_out):
    bsz, seq, _ = h.shape
    proj = h @ w_in
    q, k, v, f_logit = jnp.split(proj, [D_MODEL, 2 * D_MODEL, 3 * D_MODEL], axis=-1)
    q = q.reshape(bsz, seq, N_HEADS, HEAD_DIM)
    k = k.reshape(bsz, seq, N_HEADS, HEAD_DIM)
    v = v.reshape(bsz, seq, N_HEADS, HEAD_DIM)
    log_f = jax.nn.log_sigmoid((f_logit + b_f).astype(jnp.float32))
    cum = jnp.cumsum(log_f, axis=1).transpose(0, 2, 1)
    n_blocks = seq // Q_BLOCK
    q_blocks = q.reshape(bsz, n_blocks, Q_BLOCK, N_HEADS, HEAD_DIM).transpose(1, 0, 3, 2, 4)
    cum_q = cum.reshape(bsz, N_HEADS, n_blocks, Q_BLOCK).transpose(2, 0, 1, 3)
    k_pos = jnp.arange(seq)
    scale = HEAD_DIM ** -0.5

    def block(args):
        qi, cqi, bi = args
        s = jnp.einsum("bhqd,bshd->bhqs", qi, k,
                       preferred_element_type=jnp.float32) * scale
        s = s + cqi[..., None] - cum[:, :, None, :]
        q_pos = bi * Q_BLOCK + jnp.arange(Q_BLOCK)
        causal = k_pos[None, :] <= q_pos[:, None]
        s = jnp.where(causal[None, None], s, -jnp.inf)
        p = jax.nn.softmax(s, axis=-1).astype(v.dtype)
        return jnp.einsum("bhqs,bshd->bqhd", p, v)

    out = lax.map(block, (q_blocks, cum_q, jnp.arange(n_blocks)))
    out = out.transpose(1, 0, 2, 3, 4).reshape(bsz, seq, D_MODEL)
    return out @ w_out


def swiglu(t, w1, w3, w2):
    return (jax.nn.silu(t @ w1) * (t @ w3)) @ w2


def moe_swiglu(h, w_router, w1, w3, w2):
    bsz, seq, d = h.shape
    t = h.reshape(-1, d)
    logits = (t @ w_router).astype(jnp.float32)
    top_vals, top_idx = lax.top_k(logits, TOP_K)
    top_w = jax.nn.softmax(top_vals, axis=-1)
    gates = jnp.sum(jax.nn.one_hot(top_idx, N_EXPERTS, dtype=jnp.float32)
                    * top_w[..., None], axis=1).astype(t.dtype)
    out = jnp.zeros_like(t)
    for e in range(N_EXPERTS):
        out = out + gates[:, e:e + 1] * swiglu(t, w1[e], w3[e], w2[e])
    return out.reshape(bsz, seq, d)


def setup_inputs(seed: int = 0) -> dict:
    key = jax.random.key(seed)
    ks = jax.random.split(key, 24)
    f32 = jnp.float32
    D = D_MODEL

    def nrm(k, shape, fan_in):
        return jax.random.normal(k, shape, f32) * (fan_in ** -0.5)

    def gain(k, shape):
        return 1.0 + 0.05 * jax.random.normal(k, shape, f32)

    return {
        "x": jax.random.normal(ks[0], (BATCH, SEQ, D), f32),
        "c": jax.random.normal(ks[1], (BATCH, D), f32),
        "ada_w": nrm(ks[2], (DEPTH, D, N_MOD * D), D),
        "ada_b": 0.02 * jax.random.normal(ks[3], (DEPTH, N_MOD * D), f32),
        "norm_mix": gain(ks[4], (DEPTH, D)),
        "norm_ffn": gain(ks[5], (DEPTH, D)),
        "norm_final": gain(ks[6], (D,)),
        "conv_w_in": nrm(ks[7], (N_EVEN, D, 3 * D), D),
        "conv_kernel": nrm(ks[8], (N_EVEN, CONV_WIDTH, D), CONV_WIDTH),
        "conv_w_out": nrm(ks[9], (N_EVEN, D, D), D),
        "fox_w_in": nrm(ks[10], (N_ODD, D, 3 * D + N_HEADS), D),
        "fox_b_f": jax.random.uniform(ks[11], (N_ODD, N_HEADS), f32, 1.0, 4.0),
        "fox_w_out": nrm(ks[12], (N_ODD, D, D), D),
        "ffn_w1": nrm(ks[13], (N_EVEN, D, D_FF), D),
        "ffn_w3": nrm(ks[14], (N_EVEN, D, D_FF), D),
        "ffn_w2": nrm(ks[15], (N_EVEN, D_FF, D), D_FF),
        "moe_router": nrm(ks[16], (N_ODD, D, N_EXPERTS), D),
        "moe_w1": nrm(ks[17], (N_ODD, N_EXPERTS, D, D_FF_EXPERT), D),
        "moe_w3": nrm(ks[18], (N_ODD, N_EXPERTS, D, D_FF_EXPERT), D),
        "moe_w2": nrm(ks[19], (N_ODD, N_EXPERTS, D_FF_EXPERT, D), D_FF_EXPERT),
    }


def reference(x, c, ada_w, ada_b, norm_mix, norm_ffn, norm_final,
              conv_w_in, conv_kernel, conv_w_out,
              fox_w_in, fox_b_f, fox_w_out,
              ffn_w1, ffn_w3, ffn_w2,
              moe_router, moe_w1, moe_w3, moe_w2):
    c_act = jax.nn.silu(c)
    for i in range(DEPTH):
        j = i // 2
        mod = c_act @ ada_w[i] + ada_b[i]
        sh1, sc1, g1, sh2, sc2, g2 = jnp.split(mod, N_MOD, axis=-1)
        hn = modulate(rms_norm(x, norm_mix[i]), sh1, sc1)
        if i % N_MIXERS == 0:
            y = short_conv_mixer(hn, conv_w_in[j], conv_kernel[j], conv_w_out[j])
        else:
            y = forgetting_attention(hn, fox_w_in[j], fox_b_f[j], fox_w_out[j])
        x = x + g1[:, None, :] * y
        hn = modulate(rms_norm(x, norm_ffn[i]), sh2, sc2)
        if i % 2 == 0:
            y = swiglu(hn, ffn_w1[j], ffn_w3[j], ffn_w2[j])
        else:
            y = moe_swiglu(hn, moe_router[j], moe_w1[j], moe_w3[j], moe_w2[j])
        x = x + g2[:, None, :] * y
    return rms_norm(x, norm_final)
```

```python
import functools

import jax
import jax.numpy as jnp
from jax import lax
from jax.experimental import pallas as pl
from jax.experimental.pallas import tpu as pltpu

EPS = 1e-6
HEAD_DIM = 128
TOP_K = 2
CONV_WIDTH = 3

V7X_LANES = 128
V7X_SUBLANES = 8
V7X_SCOPED_VMEM_CAP_BYTES = 60000 * 1024
MATMUL_TEMP_BYTES = 12 * 1024 * 1024
MOE_TILE_ROWS = 512

BF16 = jnp.bfloat16
F32 = jnp.float32


def _vmem_limit(buffer_bytes):
    assert buffer_bytes + MATMUL_TEMP_BYTES <= V7X_SCOPED_VMEM_CAP_BYTES, buffer_bytes
    return V7X_SCOPED_VMEM_CAP_BYTES


def _params(semantics, buffer_bytes):
    return pltpu.CompilerParams(dimension_semantics=semantics,
                                vmem_limit_bytes=_vmem_limit(buffer_bytes))


def _dot(a, b):
    return jnp.dot(a, b, preferred_element_type=F32)


def _dot_nt(a, b):
    return lax.dot_general(a, b, (((1,), (1,)), ((), ())), preferred_element_type=F32)


def _norm_modulate(x, gain, shift, scale):
    r = lax.rsqrt(jnp.mean(x * x, axis=-1, keepdims=True) + EPS)
    return (x * r) * gain * (1.0 + scale) + shift


def _ada_kernel(c_ref, w_ref, b_ref, o_ref):
    c_act = jax.nn.silu(c_ref[...])
    o_ref[0] = _dot(c_act.astype(BF16), w_ref[0].astype(BF16)) + b_ref[0]


def _ada(c, ada_w, ada_b, *, tn=1024):
    depth, d, n = ada_w.shape
    bsz = c.shape[0]
    tn = min(tn, n)
    assert n % tn == 0
    buf = 2 * (d * tn * 4 + tn * 4 + bsz * tn * 4) + 2 * bsz * d * 4 + d * tn * 2
    return pl.pallas_call(
        _ada_kernel,
        grid=(depth, n // tn),
        in_specs=[
            pl.BlockSpec((bsz, d), lambda i, j: (0, 0)),
            pl.BlockSpec((1, d, tn), lambda i, j: (i, 0, j)),
            pl.BlockSpec((1, 1, tn), lambda i, j: (i, 0, j)),
        ],
        out_specs=pl.BlockSpec((1, bsz, tn), lambda i, j: (i, 0, j)),
        out_shape=jax.ShapeDtypeStruct((depth, bsz, n), F32),
        compiler_params=_params(("arbitrary", "arbitrary"), buf),
        name="ada",
    )(c, ada_w, ada_b.reshape(depth, 1, n))


def _conv_in_kernel(x_ref, gain_ref, sh_ref, sc_ref, wb_ref, wc_ref, wx_ref, ck_ref,
                    z_ref, hn_ref, ubuf_ref, carry_ref, *, tiles_per_seq):
    i = pl.program_id(0)
    j = pl.program_id(1)

    @pl.when(j == 0)
    def _():
        hn_ref[...] = _norm_modulate(x_ref[...], gain_ref[...], sh_ref[0], sc_ref[0]).astype(BF16)

    hn = hn_ref[...]
    b_gate = _dot(hn, wb_ref[...])
    u = _dot(hn, wc_ref[...]) * _dot(hn, wx_ref[...])
    tm = u.shape[0]
    halo = V7X_SUBLANES
    seq_start = (i % tiles_per_seq) == 0

    @pl.when(seq_start)
    def _():
        ubuf_ref[0:halo, :] = jnp.zeros((halo, u.shape[1]), F32)

    @pl.when(jnp.logical_not(seq_start))
    def _():
        ubuf_ref[0:halo, :] = carry_ref[j]

    ubuf_ref[halo:, :] = u
    carry_ref[j] = u[tm - halo:, :]
    k = ck_ref[...]
    conv = (k[0:1] * ubuf_ref[halo - 2:halo - 2 + tm, :]
            + k[1:2] * ubuf_ref[halo - 1:halo - 1 + tm, :]
            + k[2:3] * u)
    z_ref[...] = (b_gate * conv).astype(BF16)


def _conv_in(x, gain, shift, scale, w_in, conv_k, *, seq, tm=1024, tn=512):
    t, d = x.shape
    tm = min(tm, seq)
    tn = min(tn, d)
    nj = d // tn
    tiles_per_seq = seq // tm
    buf = (2 * tm * d * 4 + tm * d * 2 + 2 * 3 * d * tn * 2 + 2 * tm * tn * 2
           + (tm + V7X_SUBLANES) * tn * 4 + nj * V7X_SUBLANES * tn * 4 + 8 * d * 4)
    vec = pl.BlockSpec((1, 1, d), lambda i, j: (i // tiles_per_seq, 0, 0))
    return pl.pallas_call(
        functools.partial(_conv_in_kernel, tiles_per_seq=tiles_per_seq),
        grid=(t // tm, nj),
        in_specs=[
            pl.BlockSpec((tm, d), lambda i, j: (i, 0)),
            pl.BlockSpec((1, d), lambda i, j: (0, 0)),
            vec, vec,
            pl.BlockSpec((d, tn), lambda i, j: (0, j)),
            pl.BlockSpec((d, tn), lambda i, j: (0, nj + j)),
            pl.BlockSpec((d, tn), lambda i, j: (0, 2 * nj + j)),
            pl.BlockSpec((CONV_WIDTH, tn), lambda i, j: (0, j)),
        ],
        out_specs=pl.BlockSpec((tm, tn), lambda i, j: (i, j)),
        out_shape=jax.ShapeDtypeStruct((t, d), BF16),
        scratch_shapes=[
            pltpu.VMEM((tm, d), BF16),
            pltpu.VMEM((tm + V7X_SUBLANES, tn), F32),
            pltpu.VMEM((nj, V7X_SUBLANES, tn), F32),
        ],
        compiler_params=_params(("arbitrary", "arbitrary"), buf),
        name="conv_in",
    )(x, gain.reshape(1, d), shift, scale, w_in, w_in, w_in, conv_k)


def _proj_res_kernel(z_ref, w_ref, x_ref, g_ref, o_ref):
    o_ref[...] = x_ref[...] + g_ref[0] * _dot(z_ref[...], w_ref[...])


def _proj_res(z, w, x, gate, *, seq, tm=1024, tn=1024):
    t, d = x.shape
    tm = min(tm, seq)
    tn = min(tn, d)
    tiles_per_seq = seq // tm
    buf = 2 * (tm * d * 2 + d * tn * 2 + 2 * tm * tn * 4 + tn * 4)
    return pl.pallas_call(
        _proj_res_kernel,
        grid=(t // tm, d // tn),
        in_specs=[
            pl.BlockSpec((tm, d), lambda i, j: (i, 0)),
            pl.BlockSpec((d, tn), lambda i, j: (0, j)),
            pl.BlockSpec((tm, tn), lambda i, j: (i, j)),
            pl.BlockSpec((1, 1, tn), lambda i, j: (i // tiles_per_seq, 0, j)),
        ],
        out_specs=pl.BlockSpec((tm, tn), lambda i, j: (i, j)),
        out_shape=jax.ShapeDtypeStruct((t, d), F32),
        compiler_params=_params(("arbitrary", "arbitrary"), buf),
        name="proj_res",
    )(z, w, x, gate)


def _ffn_kernel(x_ref, gain_ref, sh_ref, sc_ref, g_ref, w1_ref, w3_ref, w2_ref,
                o_ref, hn_ref, acc_ref):
    f = pl.program_id(1)

    @pl.when(f == 0)
    def _():
        hn_ref[...] = _norm_modulate(x_ref[...], gain_ref[...], sh_ref[0], sc_ref[0]).astype(BF16)

    hn = hn_ref[...]
    h = (jax.nn.silu(_dot(hn, w1_ref[...])) * _dot(hn, w3_ref[...])).astype(BF16)
    part = _dot(h, w2_ref[...])

    @pl.when(f == 0)
    def _():
        acc_ref[...] = part

    @pl.when(f > 0)
    def _():
        acc_ref[...] += part

    @pl.when(f == pl.num_programs(1) - 1)
    def _():
        o_ref[...] = x_ref[...] + g_ref[0] * acc_ref[...]


def _ffn(x, gain, shift, scale, gate, w1, w3, w2, *, seq, tm=512, tf=512):
    t, d = x.shape
    ff = w1.shape[1]
    tm = min(tm, seq)
    tf = min(tf, ff)
    tiles_per_seq = seq // tm
    buf = (4 * tm * d * 4 + tm * d * 2 + tm * d * 4 + 2 * 3 * d * tf * 2 + 8 * d * 4)
    vec = pl.BlockSpec((1, 1, d), lambda i, f: (i // tiles_per_seq, 0, 0))
    return pl.pallas_call(
        _ffn_kernel,
        grid=(t // tm, ff // tf),
        in_specs=[
            pl.BlockSpec((tm, d), lambda i, f: (i, 0)),
            pl.BlockSpec((1, d), lambda i, f: (0, 0)),
            vec, vec, vec,
            pl.BlockSpec((d, tf), lambda i, f: (0, f)),
            pl.BlockSpec((d, tf), lambda i, f: (0, f)),
            pl.BlockSpec((tf, d), lambda i, f: (f, 0)),
        ],
        out_specs=pl.BlockSpec((tm, d), lambda i, f: (i, 0)),
        out_shape=jax.ShapeDtypeStruct((t, d), F32),
        scratch_shapes=[pltpu.VMEM((tm, d), BF16), pltpu.VMEM((tm, d), F32)],
        compiler_params=_params(("arbitrary", "arbitrary"), buf),
        name="ffn",
    )(x, gain.reshape(1, d), shift, scale, gate, w1, w3, w2)


def _fox_in_kernel(x_ref, gain_ref, sh_ref, sc_ref, w_ref, wf_ref, bf_ref,
                   qkv_ref, cum_ref, hn_ref, carry_ref, *, tiles_per_seq):
    i = pl.program_id(0)
    j = pl.program_id(1)

    @pl.when(j == 0)
    def _():
        hn = _norm_modulate(x_ref[...], gain_ref[...], sh_ref[0], sc_ref[0]).astype(BF16)
        hn_ref[...] = hn
        tm = hn.shape[0]
        nh = wf_ref.shape[0]
        log_f = jax.nn.log_sigmoid(_dot_nt(wf_ref[...], hn) + bf_ref[...])
        hi = log_f.astype(BF16)
        r1 = log_f - hi.astype(F32)
        mid = r1.astype(BF16)
        lo = (r1 - mid.astype(F32)).astype(BF16)
        row = lax.broadcasted_iota(jnp.int32, (tm, tm), 0)
        col = lax.broadcasted_iota(jnp.int32, (tm, tm), 1)
        tri = jnp.where(row <= col, 1.0, 0.0).astype(BF16)
        parts = _dot(jnp.concatenate([hi, mid, lo], axis=0), tri)
        within = parts[0:nh] + parts[nh:2 * nh] + parts[2 * nh:3 * nh]
        @pl.when((i % tiles_per_seq) == 0)
        def _():
            carry_ref[...] = jnp.zeros_like(carry_ref)

        cum = within + carry_ref[:, 0:1]
        cum_ref[0] = cum
        carry_ref[...] = jnp.broadcast_to(cum[:, tm - 1:tm], carry_ref.shape)

    qkv_ref[...] = _dot(hn_ref[...], w_ref[...]).astype(BF16)


def _fox_in(x, gain, shift, scale, w_qkv, wf_t, b_f, *, seq, tm=1024, tn=512):
    t, d = x.shape
    n = w_qkv.shape[1]
    nh = wf_t.shape[0]
    bsz = t // seq
    tm = min(tm, seq)
    tn = min(tn, n)
    tiles_per_seq = seq // tm
    buf = (2 * tm * d * 4 + tm * d * 2 + 2 * d * tn * 2 + 2 * tm * tn * 2 + tm * tm * 2
           + 2 * nh * tm * 4 + 8 * d * 4)
    vec = pl.BlockSpec((1, 1, d), lambda i, j: (i // tiles_per_seq, 0, 0))
    return pl.pallas_call(
        functools.partial(_fox_in_kernel, tiles_per_seq=tiles_per_seq),
        grid=(t // tm, n // tn),
        in_specs=[
            pl.BlockSpec((tm, d), lambda i, j: (i, 0)),
            pl.BlockSpec((1, d), lambda i, j: (0, 0)),
            vec, vec,
            pl.BlockSpec((d, tn), lambda i, j: (0, j)),
            pl.BlockSpec((nh, d), lambda i, j: (0, 0)),
            pl.BlockSpec((nh, 1), lambda i, j: (0, 0)),
        ],
        out_specs=[
            pl.BlockSpec((tm, tn), lambda i, j: (i, j)),
            pl.BlockSpec((1, nh, tm), lambda i, j: (i // tiles_per_seq, 0, i % tiles_per_seq)),
        ],
        out_shape=[jax.ShapeDtypeStruct((t, n), BF16),
                   jax.ShapeDtypeStruct((bsz, nh, seq), F32)],
        scratch_shapes=[pltpu.VMEM((tm, d), BF16), pltpu.VMEM((nh, V7X_LANES), F32)],
        compiler_params=_params(("arbitrary", "arbitrary"), buf),
        name="fox_in",
    )(x, gain.reshape(1, d), shift, scale, w_qkv, wf_t, b_f.reshape(nh, 1))


def _attn_kernel(q_ref, k_ref, v_ref, cum_ref, o_ref, *, blk):
    seq = q_ref.shape[1]
    scale = HEAD_DIM ** -0.5
    row = lax.broadcasted_iota(jnp.int32, (blk, blk), 0)
    col = lax.broadcasted_iota(jnp.int32, (blk, blk), 1)

    def kv_step(q, kj, carry, masked):
        m_prev, l_prev, acc_prev = carry
        k0 = pl.multiple_of(kj * blk, blk)
        s = _dot_nt(q, k_ref[0, pl.ds(k0, blk), :]) * scale - cum_ref[0, 0, kj]
        if masked:
            s = jnp.where(row >= col, s, -jnp.inf)
        m_new = jnp.maximum(m_prev, jnp.max(s, axis=-1, keepdims=True))
        alpha = jnp.exp(m_prev - m_new)
        p = jnp.exp(s - m_new)
        l_new = alpha * l_prev + jnp.sum(p, axis=-1, keepdims=True)
        acc_new = alpha * acc_prev + _dot(p.astype(BF16), v_ref[0, pl.ds(k0, blk), :])
        return m_new, l_new, acc_new

    def q_step(qi, _):
        q0 = pl.multiple_of(qi * blk, blk)
        q = q_ref[0, pl.ds(q0, blk), :]
        init = (jnp.full((blk, 1), -jnp.inf, F32), jnp.zeros((blk, 1), F32),
                jnp.zeros((blk, HEAD_DIM), F32))
        carry = lax.fori_loop(0, qi, lambda kj, c: kv_step(q, kj, c, False), init)
        _, l_fin, acc = kv_step(q, qi, carry, True)
        o_ref[0, pl.ds(q0, blk), :] = (acc / l_fin).astype(o_ref.dtype)
        return 0

    lax.fori_loop(0, seq // blk, q_step, 0)


def _attn(qkv, cum, *, blk=256):
    bsz, seq, n = qkv.shape
    d = n // 3
    nh = d // HEAD_DIM
    blk = min(blk, seq)
    nkv = seq // blk
    cum5 = cum.reshape(bsz, nh, nkv, 1, blk)
    buf = 2 * (4 * seq * HEAD_DIM * 2 + nkv * V7X_SUBLANES * blk * 4)
    return pl.pallas_call(
        functools.partial(_attn_kernel, blk=blk),
        grid=(bsz, nh),
        in_specs=[
            pl.BlockSpec((1, seq, HEAD_DIM), lambda b, h: (b, 0, h)),
            pl.BlockSpec((1, seq, HEAD_DIM), lambda b, h: (b, 0, nh + h)),
            pl.BlockSpec((1, seq, HEAD_DIM), lambda b, h: (b, 0, 2 * nh + h)),
            pl.BlockSpec((1, 1, nkv, 1, blk), lambda b, h: (b, h, 0, 0, 0)),
        ],
        out_specs=pl.BlockSpec((1, seq, HEAD_DIM), lambda b, h: (b, 0, h)),
        out_shape=jax.ShapeDtypeStruct((bsz, seq, d), BF16),
        compiler_params=_params(("arbitrary", "arbitrary"), buf),
        name="attn",
    )(qkv, qkv, qkv, cum5)


def _router_kernel(x_ref, gain_ref, sh_ref, sc_ref, wr_ref, hn_ref, meta_ref, *, n_experts):
    hn = _norm_modulate(x_ref[...], gain_ref[...], sh_ref[0], sc_ref[0])
    hn_ref[...] = hn
    logits = _dot(hn.astype(BF16), wr_ref[...])
    lane = lax.broadcasted_iota(jnp.int32, logits.shape, 1).astype(F32)
    lg = jnp.where(lane < n_experts, logits, -jnp.inf)
    m1 = jnp.max(lg, axis=-1, keepdims=True)
    i1 = jnp.min(jnp.where(lg == m1, lane, float(V7X_LANES)), axis=-1, keepdims=True)
    lg2 = jnp.where(lane == i1, -jnp.inf, lg)
    m2 = jnp.max(lg2, axis=-1, keepdims=True)
    i2 = jnp.min(jnp.where(lg2 == m2, lane, float(V7X_LANES)), axis=-1, keepdims=True)
    e2 = jnp.exp(m2 - m1)
    den = 1.0 + e2
    meta = jnp.where(lane == 0, i1, jnp.where(lane == 1, i2,
           jnp.where(lane == 2, 1.0 / den, jnp.where(lane == 3, e2 / den, 0.0))))
    meta_ref[...] = meta


def _router(x, gain, shift, scale, w_router_padded, *, seq, n_experts, tm=512):
    t, d = x.shape
    tm = min(tm, seq)
    tiles_per_seq = seq // tm
    buf = 4 * tm * d * 4 + 2 * d * V7X_LANES * 2 + 2 * tm * V7X_LANES * 4 + tm * d * 4
    vec = pl.BlockSpec((1, 1, d), lambda i: (i // tiles_per_seq, 0, 0))
    return pl.pallas_call(
        functools.partial(_router_kernel, n_experts=n_experts),
        grid=(t // tm,),
        in_specs=[
            pl.BlockSpec((tm, d), lambda i: (i, 0)),
            pl.BlockSpec((1, d), lambda i: (0, 0)),
            vec, vec,
            pl.BlockSpec((d, V7X_LANES), lambda i: (0, 0)),
        ],
        out_specs=[pl.BlockSpec((tm, d), lambda i: (i, 0)),
                   pl.BlockSpec((tm, V7X_LANES), lambda i: (i, 0))],
        out_shape=[jax.ShapeDtypeStruct((t, d), F32),
                   jax.ShapeDtypeStruct((t, V7X_LANES), F32)],
        compiler_params=_params(("arbitrary",), buf),
        name="router",
    )(x, gain.reshape(1, d), shift, scale, w_router_padded)


def _dispatch_plan(expert_idx, *, n_experts, tm):
    t = expert_idx.shape[0]
    n_assign = t * TOP_K
    flat = expert_idx.reshape(n_assign)
    onehot = (flat[:, None] == jnp.arange(n_experts, dtype=jnp.int32)[None, :]).astype(jnp.int32)
    csum = jnp.cumsum(onehot, axis=0)
    rank = jnp.take_along_axis(csum, flat[:, None], axis=1)[:, 0] - 1
    counts = csum[-1]
    group = ((counts + tm - 1) // tm) * tm
    group_end = jnp.cumsum(group)
    group_start = group_end - group
    dest = group_start[flat] + rank
    n_tiles = (n_assign + n_experts * (tm - 1) + tm - 1) // tm
    n_rows = n_tiles * tm
    token_of_row = jnp.zeros((n_rows,), jnp.int32).at[dest].set(
        jnp.arange(n_assign, dtype=jnp.int32) // TOP_K)
    n_used = (group_end[-1] // tm).astype(jnp.int32)
    tile_start = jnp.arange(n_tiles, dtype=jnp.int32) * tm
    tile_expert = jnp.sum((group_end[None, :] <= tile_start[:, None]).astype(jnp.int32), axis=1)
    tile_expert = jnp.minimum(tile_expert, n_experts - 1)
    last_expert = tile_expert[jnp.maximum(n_used - 1, 0)]
    tile_expert = jnp.where(jnp.arange(n_tiles) < n_used, tile_expert, last_expert)
    return (token_of_row.reshape(n_tiles, 1, tm), tile_expert.astype(jnp.int32),
            n_used.reshape(1), dest.reshape(t, TOP_K).astype(jnp.int32))


def _moe_kernel(te_ref, nused_ref, tok_ref, tok_next_ref, hn_hbm, w1_ref, w3_ref, w2_ref,
                y_ref, xg_ref, xb_ref, sem):
    i = pl.program_id(0)
    f = pl.program_id(1)
    n_used = nused_ref[0]
    tm = xb_ref.shape[0]
    valid = i < n_used
    slot = i % 2

    def start_gather(idx_ref, dst_slot):
        def body(r, _):
            tok = idx_ref[0, r]
            pltpu.make_async_copy(hn_hbm.at[pl.ds(tok, 1)], xg_ref.at[dst_slot, pl.ds(r, 1)],
                                  sem.at[dst_slot]).start()
            return 0
        lax.fori_loop(0, tm, body, 0)

    @pl.when((f == 0) & (i == 0) & valid)
    def _():
        start_gather(tok_ref, 0)

    @pl.when((f == 0) & (i + 1 < n_used))
    def _():
        start_gather(tok_next_ref, 1 - slot)

    @pl.when((f == 0) & valid)
    def _():
        pltpu.make_async_copy(hn_hbm.at[pl.ds(0, tm)], xg_ref.at[slot], sem.at[slot]).wait()
        xb_ref[...] = xg_ref[slot].astype(BF16)

    @pl.when(valid)
    def _():
        xb = xb_ref[...]
        h = (jax.nn.silu(_dot(xb, w1_ref[...])) * _dot(xb, w3_ref[...])).astype(BF16)
        part = _dot(h, w2_ref[...])

        @pl.when(f == 0)
        def _():
            y_ref[...] = part

        @pl.when(f > 0)
        def _():
            y_ref[...] += part

    @pl.when(jnp.logical_not(valid) & (f == 0))
    def _():
        y_ref[...] = jnp.zeros_like(y_ref)


def _moe(hn, w1, w3, w2, token_of_row, tile_expert, n_used, *, tf=512):
    t, d = hn.shape
    n_tiles, _, tm = token_of_row.shape
    ff = w1.shape[2]
    tf = min(tf, ff)
    nf = ff // tf
    buf = 2 * tm * d * 4 + tm * d * 2 + 2 * tm * d * 4 + 2 * 3 * d * tf * 2

    def w_col(i, f, te, nu):
        return (te[i], 0, jnp.where(i < nu[0], f, nf - 1))

    def w_row(i, f, te, nu):
        return (te[i], jnp.where(i < nu[0], f, nf - 1), 0)

    grid_spec = pltpu.PrefetchScalarGridSpec(
        num_scalar_prefetch=2,
        grid=(n_tiles, nf),
        in_specs=[
            pl.BlockSpec((None, 1, tm), lambda i, f, te, nu: (i, 0, 0), memory_space=pltpu.SMEM),
            pl.BlockSpec((None, 1, tm), lambda i, f, te, nu: (jnp.minimum(i + 1, n_tiles - 1), 0, 0),
                         memory_space=pltpu.SMEM),
            pl.BlockSpec(memory_space=pl.ANY),
            pl.BlockSpec((None, d, tf), w_col),
            pl.BlockSpec((None, d, tf), w_col),
            pl.BlockSpec((None, tf, d), w_row),
        ],
        out_specs=pl.BlockSpec((tm, d), lambda i, f, te, nu: (i, 0)),
        scratch_shapes=[
            pltpu.VMEM((2, tm, d), F32),
            pltpu.VMEM((tm, d), BF16),
            pltpu.SemaphoreType.DMA((2,)),
        ],
    )
    return pl.pallas_call(
        _moe_kernel,
        grid_spec=grid_spec,
        out_shape=jax.ShapeDtypeStruct((n_tiles * tm, d), F32),
        compiler_params=_params(("arbitrary", "arbitrary"), buf),
        name="moe",
    )(tile_expert, n_used, token_of_row, token_of_row, hn, w1, w3, w2)


def _combine_kernel(pos_ref, pos_next_ref, x_ref, g_ref, meta_ref, gain_ref, y_hbm,
                    o_ref, buf_ref, sem):
    i = pl.program_id(0)
    n = pl.num_programs(0)
    tm = x_ref.shape[0]
    slot = i % 2

    def start_gather(idx_ref, dst_slot):
        def body(r, _):
            for k in range(TOP_K):
                row = idx_ref[0, TOP_K * r + k]
                pltpu.make_async_copy(y_hbm.at[pl.ds(row, 1)], buf_ref.at[dst_slot, k, pl.ds(r, 1)],
                                      sem.at[dst_slot]).start()
            return 0
        lax.fori_loop(0, tm, body, 0)

    @pl.when(i == 0)
    def _():
        start_gather(pos_ref, 0)

    @pl.when(i + 1 < n)
    def _():
        start_gather(pos_next_ref, 1 - slot)

    for k in range(TOP_K):
        pltpu.make_async_copy(y_hbm.at[pl.ds(0, tm)], buf_ref.at[slot, k], sem.at[slot]).wait()

    meta = meta_ref[...]
    y = meta[:, 2:3] * buf_ref[slot, 0] + meta[:, 3:4] * buf_ref[slot, 1]
    xo = x_ref[...] + g_ref[0] * y
    r = lax.rsqrt(jnp.mean(xo * xo, axis=-1, keepdims=True) + EPS)
    o_ref[...] = (xo * r) * gain_ref[...]


def _combine(x, gate, meta, gain, y, pos, *, seq, tm=256):
    t, d = x.shape
    tm = min(tm, seq)
    n_tiles = t // tm
    tiles_per_seq = seq // tm
    pos3 = pos.reshape(n_tiles, 1, TOP_K * tm)
    buf = 4 * tm * d * 4 + 2 * TOP_K * tm * d * 4 + 2 * tm * V7X_LANES * 4
    return pl.pallas_call(
        _combine_kernel,
        grid=(n_tiles,),
        in_specs=[
            pl.BlockSpec((None, 1, TOP_K * tm), lambda i: (i, 0, 0), memory_space=pltpu.SMEM),
            pl.BlockSpec((None, 1, TOP_K * tm), lambda i: (jnp.minimum(i + 1, n_tiles - 1), 0, 0),
                         memory_space=pltpu.SMEM),
            pl.BlockSpec((tm, d), lambda i: (i, 0)),
            pl.BlockSpec((1, 1, d), lambda i: (i // tiles_per_seq, 0, 0)),
            pl.BlockSpec((tm, V7X_LANES), lambda i: (i, 0)),
            pl.BlockSpec((1, d), lambda i: (0, 0)),
            pl.BlockSpec(memory_space=pl.ANY),
        ],
        out_specs=pl.BlockSpec((tm, d), lambda i: (i, 0)),
        out_shape=jax.ShapeDtypeStruct((t, d), F32),
        scratch_shapes=[pltpu.VMEM((2, TOP_K, tm, d), F32), pltpu.SemaphoreType.DMA((2,))],
        compiler_params=_params(("arbitrary",), buf),
        name="combine",
    )(pos3, pos3, x, gate, meta, gain.reshape(1, d), y)


def kernel(x, c, ada_w, ada_b, norm_mix, norm_ffn, norm_final, conv_w_in, conv_kernel, conv_w_out,
           fox_w_in, fox_b_f, fox_w_out, ffn_w1, ffn_w3, ffn_w2, moe_router, moe_w1, moe_w3, moe_w2):
    bsz, seq, d = x.shape
    depth = ada_w.shape[0]
    assert depth == 2 and d % HEAD_DIM == 0
    n_experts = moe_router.shape[-1]
    t = bsz * seq
    xf = x.reshape(t, d)

    mod = _ada(c, ada_w, ada_b)

    def mod_vectors(layer):
        return [mod[layer, :, k * d:(k + 1) * d].reshape(bsz, 1, d) for k in range(6)]

    sh1, sc1, g1, sh2, sc2, g2 = mod_vectors(0)
    z = _conv_in(xf, norm_mix[0], sh1, sc1, conv_w_in[0].astype(BF16), conv_kernel[0], seq=seq)
    xf = _proj_res(z, conv_w_out[0].astype(BF16), xf, g1, seq=seq)
    xf = _ffn(xf, norm_ffn[0], sh2, sc2, g2, ffn_w1[0].astype(BF16), ffn_w3[0].astype(BF16),
              ffn_w2[0].astype(BF16), seq=seq)

    sh1, sc1, g1, sh2, sc2, g2 = mod_vectors(1)
    w_in = fox_w_in[0]
    qkv, cum = _fox_in(xf, norm_mix[1], sh1, sc1, w_in[:, :3 * d].astype(BF16),
                       w_in[:, 3 * d:].T.astype(BF16), fox_b_f[0], seq=seq)
    attn = _attn(qkv.reshape(bsz, seq, 3 * d), cum)
    xf = _proj_res(attn.reshape(t, d), fox_w_out[0].astype(BF16), xf, g1, seq=seq)

    w_router = jnp.zeros((d, V7X_LANES), BF16).at[:, :n_experts].set(moe_router[0].astype(BF16))
    hn, meta = _router(xf, norm_ffn[1], sh2, sc2, w_router, seq=seq, n_experts=n_experts)
    token_of_row, tile_expert, n_used, pos = _dispatch_plan(
        meta[:, :TOP_K].astype(jnp.int32), n_experts=n_experts, tm=min(MOE_TILE_ROWS, t))
    y = _moe(hn, moe_w1[0].astype(BF16), moe_w3[0].astype(BF16), moe_w2[0].astype(BF16),
             token_of_row, tile_expert, n_used)
    out = _combine(xf, g2, meta, norm_final, y, pos, seq=seq)
    return out.reshape(bsz, seq, d)
```

```python
import functools

import jax
import jax.numpy as jnp
from jax import lax
from jax.experimental import pallas as pl
from jax.experimental.pallas import tpu as pltpu

EPS = 1e-6
HEAD_DIM = 128
TOP_K = 2
CONV_WIDTH = 3

V7X_LANES = 128
V7X_SUBLANES = 8
V7X_SCOPED_VMEM_CAP_BYTES = 60000 * 1024
MATMUL_TEMP_BYTES = 12 * 1024 * 1024
MOE_TILE_ROWS = 1024
DMA_ISSUE_UNROLL = 8

BF16 = jnp.bfloat16
F32 = jnp.float32


def _vmem_limit(buffer_bytes):
    assert buffer_bytes + MATMUL_TEMP_BYTES <= V7X_SCOPED_VMEM_CAP_BYTES, buffer_bytes
    return V7X_SCOPED_VMEM_CAP_BYTES


def _params(semantics, buffer_bytes):
    return pltpu.CompilerParams(dimension_semantics=semantics,
                                vmem_limit_bytes=_vmem_limit(buffer_bytes))


def _dot(a, b):
    return jnp.dot(a, b, preferred_element_type=F32)


def _dot_nt(a, b):
    return lax.dot_general(a, b, (((1,), (1,)), ((), ())), preferred_element_type=F32)


def _norm_modulate(x, gain, shift, scale):
    r = lax.rsqrt(jnp.mean(x * x, axis=-1, keepdims=True) + EPS)
    return (x * r) * gain * (1.0 + scale) + shift


def _ada_kernel(c_ref, w_ref, b_ref, o_ref):
    c_act = jax.nn.silu(c_ref[...])
    o_ref[0] = _dot(c_act.astype(BF16), w_ref[0].astype(BF16)) + b_ref[0]


def _ada(c, ada_w, ada_b, *, tn=1024):
    depth, d, n = ada_w.shape
    bsz = c.shape[0]
    tn = min(tn, n)
    assert n % tn == 0
    buf = 2 * (d * tn * 4 + tn * 4 + bsz * tn * 4) + 2 * bsz * d * 4 + d * tn * 2
    return pl.pallas_call(
        _ada_kernel,
        grid=(depth, n // tn),
        in_specs=[
            pl.BlockSpec((bsz, d), lambda i, j: (0, 0)),
            pl.BlockSpec((1, d, tn), lambda i, j: (i, 0, j)),
            pl.BlockSpec((1, 1, tn), lambda i, j: (i, 0, j)),
        ],
        out_specs=pl.BlockSpec((1, bsz, tn), lambda i, j: (i, 0, j)),
        out_shape=jax.ShapeDtypeStruct((depth, bsz, n), F32),
        compiler_params=_params(("arbitrary", "arbitrary"), buf),
        name="ada",
    )(c, ada_w, ada_b.reshape(depth, 1, n))


def _conv_in_kernel(x_ref, gain_ref, sh_ref, sc_ref, wb_ref, wc_ref, wx_ref, ck_ref,
                    z_ref, hn_ref, ubuf_ref, carry_ref, *, tiles_per_seq):
    i = pl.program_id(0)
    j = pl.program_id(1)

    @pl.when(j == 0)
    def _():
        hn_ref[...] = _norm_modulate(x_ref[...], gain_ref[...], sh_ref[0], sc_ref[0]).astype(BF16)

    hn = hn_ref[...]
    b_gate = _dot(hn, wb_ref[...])
    u = _dot(hn, wc_ref[...]) * _dot(hn, wx_ref[...])
    tm = u.shape[0]
    halo = V7X_SUBLANES
    seq_start = (i % tiles_per_seq) == 0

    @pl.when(seq_start)
    def _():
        ubuf_ref[0:halo, :] = jnp.zeros((halo, u.shape[1]), F32)

    @pl.when(jnp.logical_not(seq_start))
    def _():
        ubuf_ref[0:halo, :] = carry_ref[j]

    ubuf_ref[halo:, :] = u
    carry_ref[j] = u[tm - halo:, :]
    k = ck_ref[...]
    conv = (k[0:1] * ubuf_ref[halo - 2:halo - 2 + tm, :]
            + k[1:2] * ubuf_ref[halo - 1:halo - 1 + tm, :]
            + k[2:3] * u)
    z_ref[...] = (b_gate * conv).astype(BF16)


def _conv_in(x, gain, shift, scale, w_in, conv_k, *, seq, tm=1024, tn=512):
    t, d = x.shape
    tm = min(tm, seq)
    tn = min(tn, d)
    nj = d // tn
    tiles_per_seq = seq // tm
    buf = (2 * tm * d * 4 + tm * d * 2 + 2 * 3 * d * tn * 2 + 2 * tm * tn * 2
           + (tm + V7X_SUBLANES) * tn * 4 + nj * V7X_SUBLANES * tn * 4 + 8 * d * 4)
    vec = pl.BlockSpec((1, 1, d), lambda i, j: (i // tiles_per_seq, 0, 0))
    return pl.pallas_call(
        functools.partial(_conv_in_kernel, tiles_per_seq=tiles_per_seq),
        grid=(t // tm, nj),
        in_specs=[
            pl.BlockSpec((tm, d), lambda i, j: (i, 0)),
            pl.BlockSpec((1, d), lambda i, j: (0, 0)),
            vec, vec,
            pl.BlockSpec((d, tn), lambda i, j: (0, j)),
            pl.BlockSpec((d, tn), lambda i, j: (0, nj + j)),
            pl.BlockSpec((d, tn), lambda i, j: (0, 2 * nj + j)),
            pl.BlockSpec((CONV_WIDTH, tn), lambda i, j: (0, j)),
        ],
        out_specs=pl.BlockSpec((tm, tn), lambda i, j: (i, j)),
        out_shape=jax.ShapeDtypeStruct((t, d), BF16),
        scratch_shapes=[
            pltpu.VMEM((tm, d), BF16),
            pltpu.VMEM((tm + V7X_SUBLANES, tn), F32),
            pltpu.VMEM((nj, V7X_SUBLANES, tn), F32),
        ],
        compiler_params=_params(("arbitrary", "arbitrary"), buf),
        name="conv_in",
    )(x, gain.reshape(1, d), shift, scale, w_in, w_in, w_in, conv_k)


def _proj_res_kernel(z_ref, w_ref, x_ref, g_ref, o_ref):
    o_ref[...] = x_ref[...] + g_ref[0] * _dot(z_ref[...], w_ref[...])


def _proj_res(z, w, x, gate, *, seq, tm=1024, tn=1024):
    t, d = x.shape
    tm = min(tm, seq)
    tn = min(tn, d)
    tiles_per_seq = seq // tm
    buf = 2 * (tm * d * 2 + d * tn * 2 + 2 * tm * tn * 4 + tn * 4)
    return pl.pallas_call(
        _proj_res_kernel,
        grid=(t // tm, d // tn),
        in_specs=[
            pl.BlockSpec((tm, d), lambda i, j: (i, 0)),
            pl.BlockSpec((d, tn), lambda i, j: (0, j)),
            pl.BlockSpec((tm, tn), lambda i, j: (i, j)),
            pl.BlockSpec((1, 1, tn), lambda i, j: (i // tiles_per_seq, 0, j)),
        ],
        out_specs=pl.BlockSpec((tm, tn), lambda i, j: (i, j)),
        out_shape=jax.ShapeDtypeStruct((t, d), F32),
        compiler_params=_params(("arbitrary", "arbitrary"), buf),
        name="proj_res",
    )(z, w, x, gate)


def _ffn_kernel(x_ref, gain_ref, sh_ref, sc_ref, g_ref, w1_ref, w3_ref, w2_ref,
                o_ref, hn_ref, acc_ref):
    f = pl.program_id(1)

    @pl.when(f == 0)
    def _():
        hn_ref[...] = _norm_modulate(x_ref[...], gain_ref[...], sh_ref[0], sc_ref[0]).astype(BF16)

    hn = hn_ref[...]
    h = (jax.nn.silu(_dot(hn, w1_ref[...])) * _dot(hn, w3_ref[...])).astype(BF16)
    part = _dot(h, w2_ref[...])

    @pl.when(f == 0)
    def _():
        acc_ref[...] = part

    @pl.when(f > 0)
    def _():
        acc_ref[...] += part

    @pl.when(f == pl.num_programs(1) - 1)
    def _():
        o_ref[...] = x_ref[...] + g_ref[0] * acc_ref[...]


def _ffn(x, gain, shift, scale, gate, w1, w3, w2, *, seq, tm=512, tf=512):
    t, d = x.shape
    ff = w1.shape[1]
    tm = min(tm, seq)
    tf = min(tf, ff)
    tiles_per_seq = seq // tm
    buf = (4 * tm * d * 4 + tm * d * 2 + tm * d * 4 + 2 * 3 * d * tf * 2 + 8 * d * 4)
    vec = pl.BlockSpec((1, 1, d), lambda i, f: (i // tiles_per_seq, 0, 0))
    return pl.pallas_call(
        _ffn_kernel,
        grid=(t // tm, ff // tf),
        in_specs=[
            pl.BlockSpec((tm, d), lambda i, f: (i, 0)),
            pl.BlockSpec((1, d), lambda i, f: (0, 0)),
            vec, vec, vec,
            pl.BlockSpec((d, tf), lambda i, f: (0, f)),
            pl.BlockSpec((d, tf), lambda i, f: (0, f)),
            pl.BlockSpec((tf, d), lambda i, f: (f, 0)),
        ],
        out_specs=pl.BlockSpec((tm, d), lambda i, f: (i, 0)),
        out_shape=jax.ShapeDtypeStruct((t, d), F32),
        scratch_shapes=[pltpu.VMEM((tm, d), BF16), pltpu.VMEM((tm, d), F32)],
        compiler_params=_params(("arbitrary", "arbitrary"), buf),
        name="ffn",
    )(x, gain.reshape(1, d), shift, scale, gate, w1, w3, w2)


def _fox_in_kernel(x_ref, gain_ref, sh_ref, sc_ref, w_ref, wf_ref, bf_ref,
                   qkv_ref, cum_ref, hn_ref, carry_ref, *, tiles_per_seq):
    i = pl.program_id(0)
    j = pl.program_id(1)

    @pl.when(j == 0)
    def _():
        hn = _norm_modulate(x_ref[...], gain_ref[...], sh_ref[0], sc_ref[0]).astype(BF16)
        hn_ref[...] = hn
        tm = hn.shape[0]
        nh = wf_ref.shape[0]
        log_f = jax.nn.log_sigmoid(_dot_nt(wf_ref[...], hn) + bf_ref[...])
        hi = log_f.astype(BF16)
        r1 = log_f - hi.astype(F32)
        mid = r1.astype(BF16)
        lo = (r1 - mid.astype(F32)).astype(BF16)
        row = lax.broadcasted_iota(jnp.int32, (tm, tm), 0)
        col = lax.broadcasted_iota(jnp.int32, (tm, tm), 1)
        tri = jnp.where(row <= col, 1.0, 0.0).astype(BF16)
        parts = _dot(jnp.concatenate([hi, mid, lo], axis=0), tri)
        within = parts[0:nh] + parts[nh:2 * nh] + parts[2 * nh:3 * nh]
        @pl.when((i % tiles_per_seq) == 0)
        def _():
            carry_ref[...] = jnp.zeros_like(carry_ref)

        cum = within + carry_ref[:, 0:1]
        cum_ref[0] = cum
        carry_ref[...] = jnp.broadcast_to(cum[:, tm - 1:tm], carry_ref.shape)

    qkv_ref[...] = _dot(hn_ref[...], w_ref[...]).astype(BF16)


def _fox_in(x, gain, shift, scale, w_qkv, wf_t, b_f, *, seq, tm=1024, tn=512):
    t, d = x.shape
    n = w_qkv.shape[1]
    nh = wf_t.shape[0]
    bsz = t // seq
    tm = min(tm, seq)
    tn = min(tn, n)
    tiles_per_seq = seq // tm
    buf = (2 * tm * d * 4 + tm * d * 2 + 2 * d * tn * 2 + 2 * tm * tn * 2 + tm * tm * 2
           + 2 * nh * tm * 4 + 8 * d * 4)
    vec = pl.BlockSpec((1, 1, d), lambda i, j: (i // tiles_per_seq, 0, 0))
    return pl.pallas_call(
        functools.partial(_fox_in_kernel, tiles_per_seq=tiles_per_seq),
        grid=(t // tm, n // tn),
        in_specs=[
            pl.BlockSpec((tm, d), lambda i, j: (i, 0)),
            pl.BlockSpec((1, d), lambda i, j: (0, 0)),
            vec, vec,
            pl.BlockSpec((d, tn), lambda i, j: (0, j)),
            pl.BlockSpec((nh, d), lambda i, j: (0, 0)),
            pl.BlockSpec((nh, 1), lambda i, j: (0, 0)),
        ],
        out_specs=[
            pl.BlockSpec((tm, tn), lambda i, j: (i, j)),
            pl.BlockSpec((1, nh, tm), lambda i, j: (i // tiles_per_seq, 0, i % tiles_per_seq)),
        ],
        out_shape=[jax.ShapeDtypeStruct((t, n), BF16),
                   jax.ShapeDtypeStruct((bsz, nh, seq), F32)],
        scratch_shapes=[pltpu.VMEM((tm, d), BF16), pltpu.VMEM((nh, V7X_LANES), F32)],
        compiler_params=_params(("arbitrary", "arbitrary"), buf),
        name="fox_in",
    )(x, gain.reshape(1, d), shift, scale, w_qkv, wf_t, b_f.reshape(nh, 1))


def _attn_kernel(q_ref, k_ref, v_ref, cum_ref, o_ref, *, blk):
    seq = q_ref.shape[1]
    scale = HEAD_DIM ** -0.5
    row = lax.broadcasted_iota(jnp.int32, (blk, blk), 0)
    col = lax.broadcasted_iota(jnp.int32, (blk, blk), 1)
    causal = row >= col
    for r in range(seq // blk):
        q = q_ref[0, r * blk:(r + 1) * blk, :]
        m = l = acc = None
        for j in range(r + 1):
            s = _dot_nt(q, k_ref[0, j * blk:(j + 1) * blk, :]) * scale - cum_ref[0, 0, j]
            if j == r:
                s = jnp.where(causal, s, -jnp.inf)
            m_blk = jnp.max(s, axis=-1, keepdims=True)
            v = v_ref[0, j * blk:(j + 1) * blk, :]
            if j == 0:
                m = m_blk
                p = jnp.exp(s - m)
                l = jnp.sum(p, axis=-1, keepdims=True)
                acc = _dot(p.astype(BF16), v)
            else:
                m_new = jnp.maximum(m, m_blk)
                alpha = jnp.exp(m - m_new)
                p = jnp.exp(s - m_new)
                l = alpha * l + jnp.sum(p, axis=-1, keepdims=True)
                acc = alpha * acc + _dot(p.astype(BF16), v)
                m = m_new
        o_ref[0, r * blk:(r + 1) * blk, :] = (acc / l).astype(o_ref.dtype)


def _attn(qkv, cum, *, blk=512):
    bsz, seq, n = qkv.shape
    d = n // 3
    nh = d // HEAD_DIM
    blk = min(blk, seq)
    nkv = seq // blk
    cum5 = cum.reshape(bsz, nh, nkv, 1, blk)
    buf = 2 * (4 * seq * HEAD_DIM * 2 + nkv * V7X_SUBLANES * blk * 4)
    return pl.pallas_call(
        functools.partial(_attn_kernel, blk=blk),
        grid=(bsz, nh),
        in_specs=[
            pl.BlockSpec((1, seq, HEAD_DIM), lambda b, h: (b, 0, h)),
            pl.BlockSpec((1, seq, HEAD_DIM), lambda b, h: (b, 0, nh + h)),
            pl.BlockSpec((1, seq, HEAD_DIM), lambda b, h: (b, 0, 2 * nh + h)),
            pl.BlockSpec((1, 1, nkv, 1, blk), lambda b, h: (b, h, 0, 0, 0)),
        ],
        out_specs=pl.BlockSpec((1, seq, HEAD_DIM), lambda b, h: (b, 0, h)),
        out_shape=jax.ShapeDtypeStruct((bsz, seq, d), BF16),
        compiler_params=_params(("arbitrary", "arbitrary"), buf),
        name="attn",
    )(qkv, qkv, qkv, cum5)


def _router_kernel(x_ref, gain_ref, sh_ref, sc_ref, wr_ref, hn_ref, meta_ref, *, n_experts):
    hn = _norm_modulate(x_ref[...], gain_ref[...], sh_ref[0], sc_ref[0])
    hn_ref[...] = hn
    logits = _dot(hn.astype(BF16), wr_ref[...])
    lane = lax.broadcasted_iota(jnp.int32, logits.shape, 1).astype(F32)
    lg = jnp.where(lane < n_experts, logits, -jnp.inf)
    m1 = jnp.max(lg, axis=-1, keepdims=True)
    i1 = jnp.min(jnp.where(lg == m1, lane, float(V7X_LANES)), axis=-1, keepdims=True)
    lg2 = jnp.where(lane == i1, -jnp.inf, lg)
    m2 = jnp.max(lg2, axis=-1, keepdims=True)
    i2 = jnp.min(jnp.where(lg2 == m2, lane, float(V7X_LANES)), axis=-1, keepdims=True)
    e2 = jnp.exp(m2 - m1)
    den = 1.0 + e2
    meta = jnp.where(lane == 0, i1, jnp.where(lane == 1, i2,
           jnp.where(lane == 2, 1.0 / den, jnp.where(lane == 3, e2 / den, 0.0))))
    meta_ref[...] = meta


def _router(x, gain, shift, scale, w_router_padded, *, seq, n_experts, tm=512):
    t, d = x.shape
    tm = min(tm, seq)
    tiles_per_seq = seq // tm
    buf = 4 * tm * d * 4 + 2 * d * V7X_LANES * 2 + 2 * tm * V7X_LANES * 4 + tm * d * 4
    vec = pl.BlockSpec((1, 1, d), lambda i: (i // tiles_per_seq, 0, 0))
    return pl.pallas_call(
        functools.partial(_router_kernel, n_experts=n_experts),
        grid=(t // tm,),
        in_specs=[
            pl.BlockSpec((tm, d), lambda i: (i, 0)),
            pl.BlockSpec((1, d), lambda i: (0, 0)),
            vec, vec,
            pl.BlockSpec((d, V7X_LANES), lambda i: (0, 0)),
        ],
        out_specs=[pl.BlockSpec((tm, d), lambda i: (i, 0)),
                   pl.BlockSpec((tm, V7X_LANES), lambda i: (i, 0))],
        out_shape=[jax.ShapeDtypeStruct((t, d), F32),
                   jax.ShapeDtypeStruct((t, V7X_LANES), F32)],
        compiler_params=_params(("arbitrary",), buf),
        name="router",
    )(x, gain.reshape(1, d), shift, scale, w_router_padded)


def _dispatch_plan(expert_idx, *, n_experts, tm):
    t = expert_idx.shape[0]
    n_assign = t * TOP_K
    flat = expert_idx.reshape(n_assign)
    onehot = (flat[:, None] == jnp.arange(n_experts, dtype=jnp.int32)[None, :]).astype(jnp.int32)
    csum = jnp.cumsum(onehot, axis=0)
    rank = jnp.take_along_axis(csum, flat[:, None], axis=1)[:, 0] - 1
    counts = csum[-1]
    group = ((counts + tm - 1) // tm) * tm
    group_end = jnp.cumsum(group)
    group_start = group_end - group
    dest = group_start[flat] + rank
    n_tiles = (n_assign + n_experts * (tm - 1) + tm - 1) // tm
    n_rows = n_tiles * tm
    token_of_row = jnp.zeros((n_rows,), jnp.int32).at[dest].set(
        jnp.arange(n_assign, dtype=jnp.int32) // TOP_K)
    n_used = (group_end[-1] // tm).astype(jnp.int32)
    tile_start = jnp.arange(n_tiles, dtype=jnp.int32) * tm
    tile_expert = jnp.sum((group_end[None, :] <= tile_start[:, None]).astype(jnp.int32), axis=1)
    tile_expert = jnp.minimum(tile_expert, n_experts - 1)
    last_expert = tile_expert[jnp.maximum(n_used - 1, 0)]
    tile_expert = jnp.where(jnp.arange(n_tiles) < n_used, tile_expert, last_expert)
    return (token_of_row.reshape(n_tiles, 1, tm), tile_expert.astype(jnp.int32),
            n_used.reshape(1), dest.reshape(t, TOP_K).astype(jnp.int32))


def _moe_kernel(te_ref, nused_ref, tok_ref, tok_next_ref, hn_hbm, w1_ref, w3_ref, w2_ref,
                y_ref, xg_ref, xb_ref, sem):
    i = pl.program_id(0)
    f = pl.program_id(1)
    n_used = nused_ref[0]
    tm = xb_ref.shape[0]
    valid = i < n_used

    def start_gather(idx_ref):
        def body(r, _):
            tok = idx_ref[0, r]
            pltpu.make_async_copy(hn_hbm.at[pl.ds(tok, 1)], xg_ref.at[pl.ds(r, 1)], sem.at[0]).start()
            return 0
        lax.fori_loop(0, tm, body, 0, unroll=DMA_ISSUE_UNROLL)

    @pl.when((f == 0) & (i == 0) & valid)
    def _():
        start_gather(tok_ref)

    @pl.when((f == 0) & valid)
    def _():
        pltpu.make_async_copy(hn_hbm.at[pl.ds(0, tm)], xg_ref, sem.at[0]).wait()
        xb_ref[...] = xg_ref[...].astype(BF16)

    @pl.when((f == 0) & (i + 1 < n_used))
    def _():
        start_gather(tok_next_ref)

    @pl.when(f == 0)
    def _():
        y_ref[...] = jnp.zeros_like(y_ref)

    @pl.when(valid)
    def _():
        xb = xb_ref[...]
        h = (jax.nn.silu(_dot(xb, w1_ref[...])) * _dot(xb, w3_ref[...])).astype(BF16)
        cb = h.shape[1]
        for c0 in range(0, y_ref.shape[1], cb):
            y_ref[:, c0:c0 + cb] += _dot(h, w2_ref[:, c0:c0 + cb])


def _moe(hn, w1, w3, w2, token_of_row, tile_expert, n_used, *, tf=512):
    t, d = hn.shape
    n_tiles, _, tm = token_of_row.shape
    ff = w1.shape[2]
    tf = min(tf, ff)
    nf = ff // tf
    buf = tm * d * 4 + tm * d * 2 + 2 * tm * d * 4 + 2 * 3 * d * tf * 2

    def w_col(i, f, te, nu):
        return (te[i], 0, jnp.where(i < nu[0], f, nf - 1))

    def w_row(i, f, te, nu):
        return (te[i], jnp.where(i < nu[0], f, nf - 1), 0)

    grid_spec = pltpu.PrefetchScalarGridSpec(
        num_scalar_prefetch=2,
        grid=(n_tiles, nf),
        in_specs=[
            pl.BlockSpec((None, 1, tm), lambda i, f, te, nu: (i, 0, 0), memory_space=pltpu.SMEM),
            pl.BlockSpec((None, 1, tm), lambda i, f, te, nu: (jnp.minimum(i + 1, n_tiles - 1), 0, 0),
                         memory_space=pltpu.SMEM),
            pl.BlockSpec(memory_space=pl.ANY),
            pl.BlockSpec((None, d, tf), w_col),
            pl.BlockSpec((None, d, tf), w_col),
            pl.BlockSpec((None, tf, d), w_row),
        ],
        out_specs=pl.BlockSpec((tm, d), lambda i, f, te, nu: (i, 0)),
        scratch_shapes=[
            pltpu.VMEM((tm, d), F32),
            pltpu.VMEM((tm, d), BF16),
            pltpu.SemaphoreType.DMA((1,)),
        ],
    )
    return pl.pallas_call(
        _moe_kernel,
        grid_spec=grid_spec,
        out_shape=jax.ShapeDtypeStruct((n_tiles * tm, d), F32),
        compiler_params=_params(("arbitrary", "arbitrary"), buf),
        name="moe",
    )(tile_expert, n_used, token_of_row, token_of_row, hn, w1, w3, w2)


def _combine_kernel(pos_ref, pos_next_ref, x_ref, g_ref, meta_ref, gain_ref, y_hbm,
                    o_ref, buf_ref, sem):
    i = pl.program_id(0)
    n = pl.num_programs(0)
    tm = x_ref.shape[0]
    slot = i % 2

    def start_gather(idx_ref, dst_slot):
        def body(r, _):
            for k in range(TOP_K):
                row = idx_ref[0, TOP_K * r + k]
                pltpu.make_async_copy(y_hbm.at[pl.ds(row, 1)], buf_ref.at[dst_slot, k, pl.ds(r, 1)],
                                      sem.at[dst_slot]).start()
            return 0
        lax.fori_loop(0, tm, body, 0, unroll=DMA_ISSUE_UNROLL // TOP_K)

    @pl.when(i == 0)
    def _():
        start_gather(pos_ref, 0)

    @pl.when(i + 1 < n)
    def _():
        start_gather(pos_next_ref, 1 - slot)

    for k in range(TOP_K):
        pltpu.make_async_copy(y_hbm.at[pl.ds(0, tm)], buf_ref.at[slot, k], sem.at[slot]).wait()

    meta = meta_ref[...]
    y = meta[:, 2:3] * buf_ref[slot, 0] + meta[:, 3:4] * buf_ref[slot, 1]
    xo = x_ref[...] + g_ref[0] * y
    r = lax.rsqrt(jnp.mean(xo * xo, axis=-1, keepdims=True) + EPS)
    o_ref[...] = (xo * r) * gain_ref[...]


def _combine(x, gate, meta, gain, y, pos, *, seq, tm=256):
    t, d = x.shape
    tm = min(tm, seq)
    n_tiles = t // tm
    tiles_per_seq = seq // tm
    pos3 = pos.reshape(n_tiles, 1, TOP_K * tm)
    buf = 4 * tm * d * 4 + 2 * TOP_K * tm * d * 4 + 2 * tm * V7X_LANES * 4
    return pl.pallas_call(
        _combine_kernel,
        grid=(n_tiles,),
        in_specs=[
            pl.BlockSpec((None, 1, TOP_K * tm), lambda i: (i, 0, 0), memory_space=pltpu.SMEM),
            pl.BlockSpec((None, 1, TOP_K * tm), lambda i: (jnp.minimum(i + 1, n_tiles - 1), 0, 0),
                         memory_space=pltpu.SMEM),
            pl.BlockSpec((tm, d), lambda i: (i, 0)),
            pl.BlockSpec((1, 1, d), lambda i: (i // tiles_per_seq, 0, 0)),
            pl.BlockSpec((tm, V7X_LANES), lambda i: (i, 0)),
            pl.BlockSpec((1, d), lambda i: (0, 0)),
            pl.BlockSpec(memory_space=pl.ANY),
        ],
        out_specs=pl.BlockSpec((tm, d), lambda i: (i, 0)),
        out_shape=jax.ShapeDtypeStruct((t, d), F32),
        scratch_shapes=[pltpu.VMEM((2, TOP_K, tm, d), F32), pltpu.SemaphoreType.DMA((2,))],
        compiler_params=_params(("arbitrary",), buf),
        name="combine",
    )(pos3, pos3, x, gate, meta, gain.reshape(1, d), y)


def kernel(x, c, ada_w, ada_b, norm_mix, norm_ffn, norm_final, conv_w_in, conv_kernel, conv_w_out,
           fox_w_in, fox_b_f, fox_w_out, ffn_w1, ffn_w3, ffn_w2, moe_router, moe_w1, moe_w3, moe_w2):
    bsz, seq, d = x.shape
    depth = ada_w.shape[0]
    assert depth == 2 and d % HEAD_DIM == 0
    n_experts = moe_router.shape[-1]
    t = bsz * seq
    xf = x.reshape(t, d)

    mod = _ada(c, ada_w, ada_b)

    def mod_vectors(layer):
        return [mod[layer, :, k * d:(k + 1) * d].reshape(bsz, 1, d) for k in range(6)]

    sh1, sc1, g1, sh2, sc2, g2 = mod_vectors(0)
    z = _conv_in(xf, norm_mix[0], sh1, sc1, conv_w_in[0].astype(BF16), conv_kernel[0], seq=seq)
    xf = _proj_res(z, conv_w_out[0].astype(BF16), xf, g1, seq=seq)
    xf = _ffn(xf, norm_ffn[0], sh2, sc2, g2, ffn_w1[0].astype(BF16), ffn_w3[0].astype(BF16),
              ffn_w2[0].astype(BF16), seq=seq)

    sh1, sc1, g1, sh2, sc2, g2 = mod_vectors(1)
    w_in = fox_w_in[0]
    qkv, cum = _fox_in(xf, norm_mix[1], sh1, sc1, w_in[:, :3 * d].astype(BF16),
                       w_in[:, 3 * d:].T.astype(BF16), fox_b_f[0], seq=seq)
    attn = _attn(qkv.reshape(bsz, seq, 3 * d), cum)
    xf = _proj_res(attn.reshape(t, d), fox_w_out[0].astype(BF16), xf, g1, seq=seq)

    w_router = jnp.zeros((d, V7X_LANES), BF16).at[:, :n_experts].set(moe_router[0].astype(BF16))
    hn, meta = _router(xf, norm_ffn[1], sh2, sc2, w_router, seq=seq, n_experts=n_experts)
    token_of_row, tile_expert, n_used, pos = _dispatch_plan(
        meta[:, :TOP_K].astype(jnp.int32), n_experts=n_experts, tm=min(MOE_TILE_ROWS, t))
    y = _moe(hn, moe_w1[0].astype(BF16), moe_w3[0].astype(BF16), moe_w2[0].astype(BF16),
             token_of_row, tile_expert, n_used)
    out = _combine(xf, g2, meta, norm_final, y, pos, seq=seq)
    return out.reshape(bsz, seq, d)
```

```python
import functools

import jax
import jax.numpy as jnp
from jax import lax
from jax.experimental import pallas as pl
from jax.experimental.pallas import tpu as pltpu

EPS = 1e-6
LOG2_E = 1.4426950408889634
HEAD_DIM = 128
TOP_K = 2
CONV_WIDTH = 3

V7X_LANES = 128
V7X_SUBLANES = 8
V7X_SCOPED_VMEM_CAP_BYTES = 60000 * 1024
MATMUL_TEMP_BYTES = 12 * 1024 * 1024
MOE_TILE_ROWS = 1024
DMA_ISSUE_UNROLL = 8
SWIGLU_SPLIT = 2
CONV_SPLIT = 2
V7X_BF16_SUBLANES = 16

BF16 = jnp.bfloat16
F32 = jnp.float32


def _vmem_limit(buffer_bytes):
    assert buffer_bytes + MATMUL_TEMP_BYTES <= V7X_SCOPED_VMEM_CAP_BYTES, buffer_bytes
    return V7X_SCOPED_VMEM_CAP_BYTES


def _params(semantics, buffer_bytes):
    return pltpu.CompilerParams(dimension_semantics=semantics,
                                vmem_limit_bytes=_vmem_limit(buffer_bytes))


def _dot(a, b):
    return jnp.dot(a, b, preferred_element_type=F32)


def _dot_nt(a, b):
    return lax.dot_general(a, b, (((1,), (1,)), ((), ())), preferred_element_type=F32)


def _cast_rider(w, steps, step_of):
    rows, cols = w.shape
    assert rows % V7X_BF16_SUBLANES == 0
    units = rows // V7X_BF16_SUBLANES
    n_blocks = min(steps, units)
    while units % n_blocks:
        n_blocks -= 1
    block = (rows // n_blocks, cols)

    def index_map(*grid_idx):
        return (jnp.minimum(step_of(*grid_idx), n_blocks - 1), 0)

    spec = pl.BlockSpec(block, index_map)
    return spec, spec, jax.ShapeDtypeStruct((rows, cols), BF16), 3 * 2 * block[0] * cols * 2


def _norm_modulate(x, gain, shift, scale):
    r = lax.rsqrt(jnp.mean(x * x, axis=-1, keepdims=True) + EPS)
    return (x * r) * gain * (1.0 + scale) + shift


def _ada_kernel(c_ref, w_ref, b_ref, o_ref):
    c_act = jax.nn.silu(c_ref[...])
    o_ref[0] = _dot(c_act.astype(BF16), w_ref[0].astype(BF16)) + b_ref[0]


def _ada(c, ada_w, ada_b, *, tn=1024):
    depth, d, n = ada_w.shape
    bsz = c.shape[0]
    tn = min(tn, n)
    assert n % tn == 0
    buf = 2 * (d * tn * 4 + tn * 4 + bsz * tn * 4) + 2 * bsz * d * 4 + d * tn * 2
    return pl.pallas_call(
        _ada_kernel,
        grid=(depth, n // tn),
        in_specs=[
            pl.BlockSpec((bsz, d), lambda i, j: (0, 0)),
            pl.BlockSpec((1, d, tn), lambda i, j: (i, 0, j)),
            pl.BlockSpec((1, 1, tn), lambda i, j: (i, 0, j)),
        ],
        out_specs=pl.BlockSpec((1, bsz, tn), lambda i, j: (i, 0, j)),
        out_shape=jax.ShapeDtypeStruct((depth, bsz, n), F32),
        compiler_params=_params(("arbitrary", "arbitrary"), buf),
        name="ada",
    )(c, ada_w, ada_b.reshape(depth, 1, n))


def _conv_in_kernel(x_ref, gain_ref, sh_ref, sc_ref, wb_ref, wc_ref, wx_ref, ck_ref,
                    z_ref, hn_ref, ubuf_ref, carry_ref, *, tiles_per_seq):
    i = pl.program_id(0)
    j = pl.program_id(1)

    @pl.when(j == 0)
    def _():
        hn_ref[...] = _norm_modulate(x_ref[...], gain_ref[...], sh_ref[0], sc_ref[0]).astype(BF16)

    @pl.when((i % tiles_per_seq) == 0)
    def _():
        carry_ref[j] = jnp.zeros(carry_ref.shape[1:], F32)

    hn = hn_ref[...]
    tm = hn.shape[0]
    halo = V7X_SUBLANES
    part = z_ref.shape[1] // CONV_SPLIT
    for s in range(CONV_SPLIT):
        cols = slice(s * part, (s + 1) * part)
        b_gate = _dot(hn, wb_ref[:, cols])
        u = _dot(hn, wc_ref[:, cols]) * _dot(hn, wx_ref[:, cols])
        ubuf_ref[s, 0:halo, :] = carry_ref[j, :, cols]
        ubuf_ref[s, halo:, :] = u
        carry_ref[j, :, cols] = u[tm - halo:, :]
        k = ck_ref[:, cols]
        conv = (k[0:1] * ubuf_ref[s, halo - 2:halo - 2 + tm, :]
                + k[1:2] * ubuf_ref[s, halo - 1:halo - 1 + tm, :]
                + k[2:3] * u)
        z_ref[:, cols] = (b_gate * conv).astype(BF16)


def _conv_in(x, gain, shift, scale, w_in, conv_k, *, seq, tm=1024, tn=512):
    t, d = x.shape
    tm = min(tm, seq)
    tn = min(tn, d)
    nj = d // tn
    tiles_per_seq = seq // tm
    buf = (2 * tm * d * 4 + tm * d * 2 + 2 * 3 * d * tn * 2 + 2 * tm * tn * 2
           + (tm + V7X_SUBLANES) * tn * 4 + nj * V7X_SUBLANES * tn * 4 + 8 * d * 4)
    vec = pl.BlockSpec((1, 1, d), lambda i, j: (i // tiles_per_seq, 0, 0))
    return pl.pallas_call(
        functools.partial(_conv_in_kernel, tiles_per_seq=tiles_per_seq),
        grid=(t // tm, nj),
        in_specs=[
            pl.BlockSpec((tm, d), lambda i, j: (i, 0)),
            pl.BlockSpec((1, d), lambda i, j: (0, 0)),
            vec, vec,
            pl.BlockSpec((d, tn), lambda i, j: (0, j)),
            pl.BlockSpec((d, tn), lambda i, j: (0, nj + j)),
            pl.BlockSpec((d, tn), lambda i, j: (0, 2 * nj + j)),
            pl.BlockSpec((CONV_WIDTH, tn), lambda i, j: (0, j)),
        ],
        out_specs=pl.BlockSpec((tm, tn), lambda i, j: (i, j)),
        out_shape=jax.ShapeDtypeStruct((t, d), BF16),
        scratch_shapes=[
            pltpu.VMEM((tm, d), BF16),
            pltpu.VMEM((CONV_SPLIT, tm + V7X_SUBLANES, tn // CONV_SPLIT), F32),
            pltpu.VMEM((nj, V7X_SUBLANES, tn), F32),
        ],
        compiler_params=_params(("arbitrary", "arbitrary"), buf),
        name="conv_in",
    )(x, gain.reshape(1, d), shift, scale, w_in, w_in, w_in, conv_k)


def _proj_res_kernel(z_ref, w_ref, x_ref, g_ref, o_ref):
    o_ref[...] = x_ref[...] + g_ref[0] * _dot(z_ref[...], w_ref[...])


def _proj_res(z, w, x, gate, *, seq, tm=1024, tn=1024):
    t, d = x.shape
    tm = min(tm, seq)
    tn = min(tn, d)
    tiles_per_seq = seq // tm
    buf = 2 * (tm * d * 2 + d * tn * 2 + 2 * tm * tn * 4 + tn * 4)
    return pl.pallas_call(
        _proj_res_kernel,
        grid=(t // tm, d // tn),
        in_specs=[
            pl.BlockSpec((tm, d), lambda i, j: (i, 0)),
            pl.BlockSpec((d, tn), lambda i, j: (0, j)),
            pl.BlockSpec((tm, tn), lambda i, j: (i, j)),
            pl.BlockSpec((1, 1, tn), lambda i, j: (i // tiles_per_seq, 0, j)),
        ],
        out_specs=pl.BlockSpec((tm, tn), lambda i, j: (i, j)),
        out_shape=jax.ShapeDtypeStruct((t, d), F32),
        compiler_params=_params(("arbitrary", "arbitrary"), buf),
        name="proj_res",
    )(z, w, x, gate)


def _swiglu_accumulate(xb, w1_ref, w3_ref, w2_ref, acc_ref):
    part = w1_ref.shape[1] // SWIGLU_SPLIT
    hs = []
    for s in range(SWIGLU_SPLIT):
        cols = slice(s * part, (s + 1) * part)
        hs.append((jax.nn.silu(_dot(xb, w1_ref[:, cols])) * _dot(xb, w3_ref[:, cols])).astype(BF16))
    cb = w1_ref.shape[1]
    for s in range(SWIGLU_SPLIT):
        rows = slice(s * part, (s + 1) * part)
        for c0 in range(0, acc_ref.shape[1], cb):
            acc_ref[:, c0:c0 + cb] += _dot(hs[s], w2_ref[rows, c0:c0 + cb])


def _ffn_kernel(x_ref, gain_ref, sh_ref, sc_ref, g_ref, w1_ref, w3_ref, w2_ref, ride_ref,
                o_ref, ride_out_ref, hn_ref, acc_ref):
    f = pl.program_id(1)
    ride_out_ref[...] = ride_ref[...].astype(BF16)

    @pl.when(f == 0)
    def _():
        hn_ref[...] = _norm_modulate(x_ref[...], gain_ref[...], sh_ref[0], sc_ref[0]).astype(BF16)
        acc_ref[...] = jnp.zeros_like(acc_ref)

    _swiglu_accumulate(hn_ref[...], w1_ref, w3_ref, w2_ref, acc_ref)

    @pl.when(f == pl.num_programs(1) - 1)
    def _():
        o_ref[...] = x_ref[...] + g_ref[0] * acc_ref[...]


def _ffn(x, gain, shift, scale, gate, w1, w3, w2, ride, *, seq, tm=512, tf=512):
    t, d = x.shape
    ff = w1.shape[1]
    tm = min(tm, seq)
    tf = min(tf, ff)
    nf = ff // tf
    tiles_per_seq = seq // tm
    ride_in, ride_out, ride_shape, ride_bytes = _cast_rider(ride, (t // tm) * nf, lambda i, f: i * nf + f)
    buf = (4 * tm * d * 4 + tm * d * 2 + tm * d * 4 + 2 * 3 * d * tf * 2 + 8 * d * 4 + ride_bytes)
    vec = pl.BlockSpec((1, 1, d), lambda i, f: (i // tiles_per_seq, 0, 0))
    return pl.pallas_call(
        _ffn_kernel,
        grid=(t // tm, nf),
        in_specs=[
            pl.BlockSpec((tm, d), lambda i, f: (i, 0)),
            pl.BlockSpec((1, d), lambda i, f: (0, 0)),
            vec, vec, vec,
            pl.BlockSpec((d, tf), lambda i, f: (0, f)),
            pl.BlockSpec((d, tf), lambda i, f: (0, f)),
            pl.BlockSpec((tf, d), lambda i, f: (f, 0)),
            ride_in,
        ],
        out_specs=[pl.BlockSpec((tm, d), lambda i, f: (i, 0)), ride_out],
        out_shape=[jax.ShapeDtypeStruct((t, d), F32), ride_shape],
        scratch_shapes=[pltpu.VMEM((tm, d), BF16), pltpu.VMEM((tm, d), F32)],
        compiler_params=_params(("arbitrary", "arbitrary"), buf),
        name="ffn",
    )(x, gain.reshape(1, d), shift, scale, gate, w1, w3, w2, ride)


def _fox_in_kernel(x_ref, gain_ref, sh_ref, sc_ref, w_ref, wf_ref, bf_ref, ride_ref,
                   qkv_ref, cum_ref, ride_out_ref, hn_ref, carry_ref, *, tiles_per_seq, q_blocks):
    i = pl.program_id(0)
    j = pl.program_id(1)
    ride_out_ref[...] = ride_ref[...].astype(BF16)

    @pl.when(j == 0)
    def _():
        hn = _norm_modulate(x_ref[...], gain_ref[...], sh_ref[0], sc_ref[0]).astype(BF16)
        hn_ref[...] = hn
        tm = hn.shape[0]
        nh = wf_ref.shape[0]
        log_f = jax.nn.log_sigmoid(_dot_nt(wf_ref[...], hn) + bf_ref[...])
        hi = log_f.astype(BF16)
        r1 = log_f - hi.astype(F32)
        mid = r1.astype(BF16)
        lo = (r1 - mid.astype(F32)).astype(BF16)
        row = lax.broadcasted_iota(jnp.int32, (tm, tm), 0)
        col = lax.broadcasted_iota(jnp.int32, (tm, tm), 1)
        tri = jnp.where(row <= col, 1.0, 0.0).astype(BF16)
        parts = _dot(jnp.concatenate([hi, mid, lo], axis=0), tri)
        within = parts[0:nh] + parts[nh:2 * nh] + parts[2 * nh:3 * nh]

        @pl.when((i % tiles_per_seq) == 0)
        def _():
            carry_ref[...] = jnp.zeros_like(carry_ref)

        cum = within + carry_ref[:, 0:1]
        cum_ref[0] = cum * LOG2_E
        carry_ref[...] = jnp.broadcast_to(cum[:, tm - 1:tm], carry_ref.shape)

    col_scale = jnp.where(j < q_blocks, HEAD_DIM ** -0.5 * LOG2_E, 1.0)
    qkv_ref[...] = (_dot(hn_ref[...], w_ref[...]) * col_scale).astype(BF16)


def _fox_in(x, gain, shift, scale, w_qkv, wf_t, b_f, ride, *, seq, tm=1024, tn=1024):
    t, d = x.shape
    n = w_qkv.shape[1]
    nh = wf_t.shape[0]
    bsz = t // seq
    tm = min(tm, seq)
    tn = min(tn, d)
    assert d % tn == 0 and n == 3 * d
    tiles_per_seq = seq // tm
    nj = n // tn
    ride_in, ride_out, ride_shape, ride_bytes = _cast_rider(ride, (t // tm) * nj, lambda i, j: i * nj + j)
    buf = (2 * tm * d * 4 + tm * d * 2 + 2 * d * tn * 2 + 2 * tm * tn * 2 + tm * tm * 2
           + 2 * nh * tm * 4 + 8 * d * 4 + ride_bytes)
    vec = pl.BlockSpec((1, 1, d), lambda i, j: (i // tiles_per_seq, 0, 0))
    return pl.pallas_call(
        functools.partial(_fox_in_kernel, tiles_per_seq=tiles_per_seq, q_blocks=d // tn),
        grid=(t // tm, nj),
        in_specs=[
            pl.BlockSpec((tm, d), lambda i, j: (i, 0)),
            pl.BlockSpec((1, d), lambda i, j: (0, 0)),
            vec, vec,
            pl.BlockSpec((d, tn), lambda i, j: (0, j)),
            pl.BlockSpec((nh, d), lambda i, j: (0, 0)),
            pl.BlockSpec((nh, 1), lambda i, j: (0, 0)),
            ride_in,
        ],
        out_specs=[
            pl.BlockSpec((tm, tn), lambda i, j: (i, j)),
            pl.BlockSpec((1, nh, tm), lambda i, j: (i // tiles_per_seq, 0, i % tiles_per_seq)),
            ride_out,
        ],
        out_shape=[jax.ShapeDtypeStruct((t, n), BF16),
                   jax.ShapeDtypeStruct((bsz, nh, seq), F32), ride_shape],
        scratch_shapes=[pltpu.VMEM((tm, d), BF16), pltpu.VMEM((nh, V7X_LANES), F32)],
        compiler_params=_params(("arbitrary", "arbitrary"), buf),
        name="fox_in",
    )(x, gain.reshape(1, d), shift, scale, w_qkv, wf_t, b_f.reshape(nh, 1), ride)


def _attn_kernel(q_ref, k_ref, v_ref, cum_ref, ride_ref, o_ref, ride_out_ref, *, blk):
    seq = q_ref.shape[1]
    ride_out_ref[...] = ride_ref[...].astype(BF16)
    row = lax.broadcasted_iota(jnp.int32, (blk, blk), 0)
    col = lax.broadcasted_iota(jnp.int32, (blk, blk), 1)
    causal = row >= col
    for r in range(seq // blk):
        q = q_ref[0, r * blk:(r + 1) * blk, :]
        m = l = acc = None
        for j in range(r + 1):
            s = _dot_nt(q, k_ref[0, j * blk:(j + 1) * blk, :]) - cum_ref[0, 0, j]
            if j == r:
                s = jnp.where(causal, s, -jnp.inf)
            m_blk = jnp.max(s, axis=-1, keepdims=True)
            v = v_ref[0, j * blk:(j + 1) * blk, :]
            if j == 0:
                m = m_blk
                p = jnp.exp2(s - m)
                l = jnp.sum(p, axis=-1, keepdims=True)
                acc = _dot(p.astype(BF16), v)
            else:
                m_new = jnp.maximum(m, m_blk)
                alpha = jnp.exp2(m - m_new)
                p = jnp.exp2(s - m_new)
                l = alpha * l + jnp.sum(p, axis=-1, keepdims=True)
                acc = alpha * acc + _dot(p.astype(BF16), v)
                m = m_new
        o_ref[0, r * blk:(r + 1) * blk, :] = (acc / l).astype(o_ref.dtype)


def _attn(qkv, cum, ride, *, blk=512):
    bsz, seq, n = qkv.shape
    d = n // 3
    nh = d // HEAD_DIM
    blk = min(blk, seq)
    nkv = seq // blk
    cum5 = cum.reshape(bsz, nh, nkv, 1, blk)
    ride_in, ride_out, ride_shape, ride_bytes = _cast_rider(ride, bsz * nh, lambda b, h: b * nh + h)
    buf = 2 * (4 * seq * HEAD_DIM * 2 + nkv * V7X_SUBLANES * blk * 4) + ride_bytes
    return pl.pallas_call(
        functools.partial(_attn_kernel, blk=blk),
        grid=(bsz, nh),
        in_specs=[
            pl.BlockSpec((1, seq, HEAD_DIM), lambda b, h: (b, 0, h)),
            pl.BlockSpec((1, seq, HEAD_DIM), lambda b, h: (b, 0, nh + h)),
            pl.BlockSpec((1, seq, HEAD_DIM), lambda b, h: (b, 0, 2 * nh + h)),
            pl.BlockSpec((1, 1, nkv, 1, blk), lambda b, h: (b, h, 0, 0, 0)),
            ride_in,
        ],
        out_specs=[pl.BlockSpec((1, seq, HEAD_DIM), lambda b, h: (b, 0, h)), ride_out],
        out_shape=[jax.ShapeDtypeStruct((bsz, seq, d), BF16), ride_shape],
        compiler_params=_params(("arbitrary", "arbitrary"), buf),
        name="attn",
    )(qkv, qkv, qkv, cum5, ride)


def _router_kernel(x_ref, gain_ref, sh_ref, sc_ref, wr_ref, hn_ref, meta_ref, *, n_experts):
    hn = _norm_modulate(x_ref[...], gain_ref[...], sh_ref[0], sc_ref[0])
    hn_ref[...] = hn
    logits = _dot(hn.astype(BF16), wr_ref[...])
    lane = lax.broadcasted_iota(jnp.int32, logits.shape, 1).astype(F32)
    lg = jnp.where(lane < n_experts, logits, -jnp.inf)
    m1 = jnp.max(lg, axis=-1, keepdims=True)
    i1 = jnp.min(jnp.where(lg == m1, lane, float(V7X_LANES)), axis=-1, keepdims=True)
    lg2 = jnp.where(lane == i1, -jnp.inf, lg)
    m2 = jnp.max(lg2, axis=-1, keepdims=True)
    i2 = jnp.min(jnp.where(lg2 == m2, lane, float(V7X_LANES)), axis=-1, keepdims=True)
    e2 = jnp.exp(m2 - m1)
    den = 1.0 + e2
    meta = jnp.where(lane == 0, i1, jnp.where(lane == 1, i2,
           jnp.where(lane == 2, 1.0 / den, jnp.where(lane == 3, e2 / den, 0.0))))
    meta_ref[...] = meta


def _router(x, gain, shift, scale, w_router_padded, *, seq, n_experts, tm=512):
    t, d = x.shape
    tm = min(tm, seq)
    tiles_per_seq = seq // tm
    buf = 4 * tm * d * 4 + 2 * d * V7X_LANES * 2 + 2 * tm * V7X_LANES * 4 + tm * d * 4
    vec = pl.BlockSpec((1, 1, d), lambda i: (i // tiles_per_seq, 0, 0))
    return pl.pallas_call(
        functools.partial(_router_kernel, n_experts=n_experts),
        grid=(t // tm,),
        in_specs=[
            pl.BlockSpec((tm, d), lambda i: (i, 0)),
            pl.BlockSpec((1, d), lambda i: (0, 0)),
            vec, vec,
            pl.BlockSpec((d, V7X_LANES), lambda i: (0, 0)),
        ],
        out_specs=[pl.BlockSpec((tm, d), lambda i: (i, 0)),
                   pl.BlockSpec((tm, V7X_LANES), lambda i: (i, 0))],
        out_shape=[jax.ShapeDtypeStruct((t, d), F32),
                   jax.ShapeDtypeStruct((t, V7X_LANES), F32)],
        compiler_params=_params(("arbitrary",), buf),
        name="router",
    )(x, gain.reshape(1, d), shift, scale, w_router_padded)


def _dispatch_plan(expert_idx, *, n_experts, tm):
    t = expert_idx.shape[0]
    n_assign = t * TOP_K
    flat = expert_idx.reshape(n_assign)
    onehot = (flat[:, None] == jnp.arange(n_experts, dtype=jnp.int32)[None, :]).astype(jnp.int32)
    csum = jnp.cumsum(onehot, axis=0)
    rank = jnp.take_along_axis(csum, flat[:, None], axis=1)[:, 0] - 1
    counts = csum[-1]
    group = ((counts + tm - 1) // tm) * tm
    group_end = jnp.cumsum(group)
    group_start = group_end - group
    dest = group_start[flat] + rank
    n_tiles = (n_assign + n_experts * (tm - 1) + tm - 1) // tm
    n_rows = n_tiles * tm
    token_of_row = jnp.zeros((n_rows,), jnp.int32).at[dest].set(
        jnp.arange(n_assign, dtype=jnp.int32) // TOP_K)
    n_used = (group_end[-1] // tm).astype(jnp.int32)
    tile_start = jnp.arange(n_tiles, dtype=jnp.int32) * tm
    tile_expert = jnp.sum((group_end[None, :] <= tile_start[:, None]).astype(jnp.int32), axis=1)
    tile_expert = jnp.minimum(tile_expert, n_experts - 1)
    last_expert = tile_expert[jnp.maximum(n_used - 1, 0)]
    tile_expert = jnp.where(jnp.arange(n_tiles) < n_used, tile_expert, last_expert)
    return (token_of_row.reshape(n_tiles, 1, tm), tile_expert.astype(jnp.int32),
            n_used.reshape(1), dest.reshape(t, TOP_K).astype(jnp.int32))


def _moe_kernel(te_ref, nused_ref, tok_ref, tok_next_ref, hn_hbm, w1_ref, w3_ref, w2_ref,
                y_ref, xg_ref, xb_ref, sem, *, rows_per_step):
    i = pl.program_id(0)
    f = pl.program_id(1)
    n_used = nused_ref[0]
    n_tiles = pl.num_programs(0)
    nf = pl.num_programs(1)
    tm = xb_ref.shape[0]
    n_issue = xg_ref.shape[0]
    valid = i < n_used

    def row_copy(idx_ref, r):
        tok = idx_ref[0, jnp.minimum(r, tm - 1)]
        return pltpu.make_async_copy(hn_hbm.at[pl.ds(tok, 1)], xg_ref.at[pl.ds(r, 1)], sem.at[0])

    def wait_tile():
        pltpu.make_async_copy(hn_hbm.at[pl.ds(0, tm)], xg_ref.at[pl.ds(0, tm)], sem.at[0]).wait()
        if n_issue > tm:
            pltpu.make_async_copy(hn_hbm.at[pl.ds(0, n_issue - tm)],
                                  xg_ref.at[pl.ds(tm, n_issue - tm)], sem.at[0]).wait()

    @pl.when((f == 0) & (i == 0) & valid)
    def _():
        def body(r, carry):
            row_copy(tok_ref, r).start()
            return carry
        lax.fori_loop(0, n_issue, body, 0)

    @pl.when((f == 0) & valid)
    def _():
        wait_tile()
        xb_ref[...] = xg_ref[0:tm, :].astype(BF16)

    @pl.when((f == 0) & (i == n_used) & (i > 0))
    def _():
        wait_tile()

    @pl.when(f == 0)
    def _():
        y_ref[...] = jnp.zeros_like(y_ref)

    @pl.when(valid)
    def _():
        for u in range(rows_per_step):
            row_copy(tok_next_ref, f * rows_per_step + u).start()
        _swiglu_accumulate(xb_ref[...], w1_ref, w3_ref, w2_ref, y_ref)

    @pl.when(valid & (i == n_tiles - 1) & (f == nf - 1))
    def _():
        wait_tile()


def _moe(hn, w1, w3, w2, token_of_row, tile_expert, n_used, *, tf=512):
    t, d = hn.shape
    n_tiles, _, tm = token_of_row.shape
    ff = w1.shape[2]
    tf = min(tf, ff)
    nf = ff // tf
    rows_per_step = -(-tm // nf)
    while (rows_per_step * nf - tm) % V7X_SUBLANES:
        rows_per_step += 1
    n_issue = rows_per_step * nf
    buf = n_issue * d * 4 + tm * d * 2 + 2 * tm * d * 4 + 2 * 3 * d * tf * 2

    def w_col(i, f, te, nu):
        return (te[i], 0, jnp.where(i < nu[0], f, nf - 1))

    def w_row(i, f, te, nu):
        return (te[i], jnp.where(i < nu[0], f, nf - 1), 0)

    grid_spec = pltpu.PrefetchScalarGridSpec(
        num_scalar_prefetch=2,
        grid=(n_tiles, nf),
        in_specs=[
            pl.BlockSpec((None, 1, tm), lambda i, f, te, nu: (i, 0, 0), memory_space=pltpu.SMEM),
            pl.BlockSpec((None, 1, tm), lambda i, f, te, nu: (jnp.minimum(i + 1, n_tiles - 1), 0, 0),
                         memory_space=pltpu.SMEM),
            pl.BlockSpec(memory_space=pl.ANY),
            pl.BlockSpec((None, d, tf), w_col),
            pl.BlockSpec((None, d, tf), w_col),
            pl.BlockSpec((None, tf, d), w_row),
        ],
        out_specs=pl.BlockSpec((tm, d), lambda i, f, te, nu: (i, 0)),
        scratch_shapes=[
            pltpu.VMEM((n_issue, d), F32),
            pltpu.VMEM((tm, d), BF16),
            pltpu.SemaphoreType.DMA((1,)),
        ],
    )
    return pl.pallas_call(
        functools.partial(_moe_kernel, rows_per_step=rows_per_step),
        grid_spec=grid_spec,
        out_shape=jax.ShapeDtypeStruct((n_tiles * tm, d), F32),
        compiler_params=_params(("arbitrary", "arbitrary"), buf),
        name="moe",
    )(tile_expert, n_used, token_of_row, token_of_row, hn, w1, w3, w2)


def _combine_kernel(pos_ref, pos_next_ref, x_ref, g_ref, meta_ref, gain_ref, y_hbm,
                    o_ref, buf_ref, sem):
    i = pl.program_id(0)
    n = pl.num_programs(0)
    tm = x_ref.shape[0]
    slot = i % 2

    def start_gather(idx_ref, dst_slot):
        def body(r, _):
            for k in range(TOP_K):
                row = idx_ref[0, TOP_K * r + k]
                pltpu.make_async_copy(y_hbm.at[pl.ds(row, 1)], buf_ref.at[dst_slot, k, pl.ds(r, 1)],
                                      sem.at[dst_slot]).start()
            return 0
        lax.fori_loop(0, tm, body, 0, unroll=DMA_ISSUE_UNROLL // TOP_K)

    @pl.when(i == 0)
    def _():
        start_gather(pos_ref, 0)

    @pl.when(i + 1 < n)
    def _():
        start_gather(pos_next_ref, 1 - slot)

    for k in range(TOP_K):
        pltpu.make_async_copy(y_hbm.at[pl.ds(0, tm)], buf_ref.at[slot, k], sem.at[slot]).wait()

    meta = meta_ref[...]
    y = meta[:, 2:3] * buf_ref[slot, 0] + meta[:, 3:4] * buf_ref[slot, 1]
    xo = x_ref[...] + g_ref[0] * y
    r = lax.rsqrt(jnp.mean(xo * xo, axis=-1, keepdims=True) + EPS)
    o_ref[...] = (xo * r) * gain_ref[...]


def _combine(x, gate, meta, gain, y, pos, *, seq, tm=256):
    t, d = x.shape
    tm = min(tm, seq)
    n_tiles = t // tm
    tiles_per_seq = seq // tm
    pos3 = pos.reshape(n_tiles, 1, TOP_K * tm)
    buf = 4 * tm * d * 4 + 2 * TOP_K * tm * d * 4 + 2 * tm * V7X_LANES * 4
    return pl.pallas_call(
        _combine_kernel,
        grid=(n_tiles,),
        in_specs=[
            pl.BlockSpec((None, 1, TOP_K * tm), lambda i: (i, 0, 0), memory_space=pltpu.SMEM),
            pl.BlockSpec((None, 1, TOP_K * tm), lambda i: (jnp.minimum(i + 1, n_tiles - 1), 0, 0),
                         memory_space=pltpu.SMEM),
            pl.BlockSpec((tm, d), lambda i: (i, 0)),
            pl.BlockSpec((1, 1, d), lambda i: (i // tiles_per_seq, 0, 0)),
            pl.BlockSpec((tm, V7X_LANES), lambda i: (i, 0)),
            pl.BlockSpec((1, d), lambda i: (0, 0)),
            pl.BlockSpec(memory_space=pl.ANY),
        ],
        out_specs=pl.BlockSpec((tm, d), lambda i: (i, 0)),
        out_shape=jax.ShapeDtypeStruct((t, d), F32),
        scratch_shapes=[pltpu.VMEM((2, TOP_K, tm, d), F32), pltpu.SemaphoreType.DMA((2,))],
        compiler_params=_params(("arbitrary",), buf),
        name="combine",
    )(pos3, pos3, x, gate, meta, gain.reshape(1, d), y)


def kernel(x, c, ada_w, ada_b, norm_mix, norm_ffn, norm_final, conv_w_in, conv_kernel, conv_w_out,
           fox_w_in, fox_b_f, fox_w_out, ffn_w1, ffn_w3, ffn_w2, moe_router, moe_w1, moe_w3, moe_w2):
    bsz, seq, d = x.shape
    depth = ada_w.shape[0]
    assert depth == 2 and d % HEAD_DIM == 0
    n_experts = moe_router.shape[-1]
    t = bsz * seq
    xf = x.reshape(t, d)

    mod = _ada(c, ada_w, ada_b)

    def mod_vectors(layer):
        return [mod[layer, :, k * d:(k + 1) * d].reshape(bsz, 1, d) for k in range(6)]

    sh1, sc1, g1, sh2, sc2, g2 = mod_vectors(0)
    z = _conv_in(xf, norm_mix[0], sh1, sc1, conv_w_in[0].astype(BF16), conv_kernel[0], seq=seq)
    xf = _proj_res(z, conv_w_out[0].astype(BF16), xf, g1, seq=seq)
    ff_e = moe_w1.shape[-1]
    xf, w3_e = _ffn(xf, norm_ffn[0], sh2, sc2, g2, ffn_w1[0].astype(BF16), ffn_w3[0].astype(BF16),
                    ffn_w2[0].astype(BF16), moe_w3[0].reshape(n_experts * d, ff_e), seq=seq)

    sh1, sc1, g1, sh2, sc2, g2 = mod_vectors(1)
    w_in = fox_w_in[0]
    qkv, cum, w2_e = _fox_in(xf, norm_mix[1], sh1, sc1, w_in[:, :3 * d].astype(BF16),
                             w_in[:, 3 * d:].T.astype(BF16), fox_b_f[0],
                             moe_w2[0].reshape(n_experts * ff_e, d), seq=seq)
    attn, w1_e = _attn(qkv.reshape(bsz, seq, 3 * d), cum, moe_w1[0].reshape(n_experts * d, ff_e))
    xf = _proj_res(attn.reshape(t, d), fox_w_out[0].astype(BF16), xf, g1, seq=seq)

    w_router = jnp.zeros((d, V7X_LANES), BF16).at[:, :n_experts].set(moe_router[0].astype(BF16))
    hn, meta = _router(xf, norm_ffn[1], sh2, sc2, w_router, seq=seq, n_experts=n_experts)
    token_of_row, tile_expert, n_used, pos = _dispatch_plan(
        meta[:, :TOP_K].astype(jnp.int32), n_experts=n_experts, tm=min(MOE_TILE_ROWS, t))
    y = _moe(hn, w1_e.reshape(n_experts, d, ff_e), w3_e.reshape(n_experts, d, ff_e),
             w2_e.reshape(n_experts, ff_e, d), token_of_row, tile_expert, n_used)
    out = _combine(xf, g2, meta, norm_final, y, pos, seq=seq)
    return out.reshape(bsz, seq, d)
```
